```python
import functools
import jax
import jax.numpy as jnp
from jax import lax
import numpy as np

D_MODEL = 1024
BATCH = 4
SEQ = 4096
DEPTH = 1
DEC_BATCH = 128
DEC_SEQ = 8
PAST_LEN = 8192
PAGE_SIZE = 128

N_HEADS = 16
N_KV_HEADS = 2
HEAD_DIM = 64
GROUP = N_HEADS // N_KV_HEADS
WINDOW = 128
ATTN_BLOCK = WINDOW
Q_W = N_HEADS * HEAD_DIM
KV_W = N_KV_HEADS * HEAD_DIM
CONV_DIM = D_MODEL
CONV_WIDTH = 3
IN_SPLITS = (
    Q_W,
    Q_W + KV_W,
    Q_W + 2 * KV_W,
    Q_W + 2 * KV_W + CONV_DIM,
    Q_W + 2 * KV_W + 2 * CONV_DIM,
    Q_W + 2 * KV_W + 3 * CONV_DIM,
    Q_W + 2 * KV_W + 3 * CONV_DIM + D_MODEL,
)
IN_COLS = Q_W + 2 * KV_W + 3 * CONV_DIM + 2 * D_MODEL
N_EXPERTS = 64
TOP_K = 6
N_GROUPS = 8
TOPK_GROUPS = 4
EXPERT_FF = 256
SHARED_FF = 256
ROUTED_SCALE = 2.5
MOE_BLOCK = 128
PLE_DIM = 256
LN_EPS = 1e-5
DN_ALPHA = (2 * DEPTH) ** 0.25
DN_BETA = (8 * DEPTH) ** -0.25

kernel_name = "hybrid_swa_sink_shortconv_moe_deepnorm_step"


def layer_norm(x, g, b):
    xf = x.astype(jnp.float32)
    mu = jnp.mean(xf, axis=-1, keepdims=True)
    xc = xf - mu
    var = jnp.mean(xc * xc, axis=-1, keepdims=True)
    return (xc * lax.rsqrt(var + LN_EPS) * g + b).astype(x.dtype)


def sink_softmax(logits, valid, sinks):
    s = sinks.astype(jnp.float32).reshape(N_KV_HEADS, GROUP)[:, :, None, None]
    logits = jnp.where(valid, logits, -jnp.inf)
    m = jnp.maximum(jnp.max(logits, axis=-1, keepdims=True), s)
    e = jnp.where(valid, jnp.exp(logits - m), 0.0)
    return e / (jnp.sum(e, axis=-1, keepdims=True) + jnp.exp(s - m))


def swa_prompt(q, k, v, sinks):
    n, s = q.shape[:2]
    nb = s // ATTN_BLOCK
    qb = q.reshape(n, nb, ATTN_BLOCK, N_KV_HEADS, GROUP, HEAD_DIM)
    kb = k.reshape(n, nb, ATTN_BLOCK, N_KV_HEADS, HEAD_DIM)
    vb = v.reshape(n, nb, ATTN_BLOCK, N_KV_HEADS, HEAD_DIM)

    def with_prev(t):
        prev = jnp.concatenate([jnp.zeros_like(t[:, :1]), t[:, :-1]], axis=1)
        return jnp.concatenate([prev, t], axis=2)

    kk, vv = with_prev(kb), with_prev(vb)
    logits = jnp.einsum('nbqkgd,nbskd->nbkgqs', qb, kk,
                        preferred_element_type=jnp.float32) * (HEAD_DIM ** -0.5)
    qpos = jnp.arange(ATTN_BLOCK)[:, None] + ATTN_BLOCK
    kpos = jnp.arange(2 * ATTN_BLOCK)[None, :]
    rel = qpos - kpos
    band = (rel >= 0) & (rel < WINDOW)
    real = (jnp.arange(nb)[:, None] > 0) | (kpos >= ATTN_BLOCK)
    valid = (band[None] & real[:, None, :])[None, :, None, None]
    p = sink_softmax(logits, valid, sinks).astype(v.dtype)
    out = jnp.einsum('nbkgqs,nbskd->nbqkgd', p, vv).reshape(n, s, Q_W)
    wc = min(WINDOW, s)
    return out, k[:, s - wc:], v[:, s - wc:]


def swa_sample(q, k, v, sinks, cache_k, cache_v):
    n, s = q.shape[:2]
    wc = cache_k.shape[1]
    kk = jnp.concatenate([cache_k, k], axis=1)
    vv = jnp.concatenate([cache_v, v], axis=1)
    qg = q.reshape(n, s, N_KV_HEADS, GROUP, HEAD_DIM)
    logits = jnp.einsum('nqkgd,nskd->nkgqs', qg, kk,
                        preferred_element_type=jnp.float32) * (HEAD_DIM ** -0.5)
    rel = jnp.arange(s)[:, None] - (jnp.arange(wc + s)[None, :] - wc)
    valid = (rel >= 0) & (rel < WINDOW)
    p = sink_softmax(logits, valid, sinks).astype(v.dtype)
    out = jnp.einsum('nkgqs,nskd->nqkgd', p, vv).reshape(n, s, Q_W)
    return out, kk[:, s:], vv[:, s:]


def short_conv(u, prev, w):
    s = u.shape[1]
    full = jnp.concatenate([prev, u], axis=1)
    y = full[:, 0:s] * w[0]
    for j in range(1, CONV_WIDTH):
        y = y + full[:, j:j + s] * w[j]
    return y, full[:, s:]


def swiglu(x, wg, wu, wd):
    return (jax.nn.silu(x @ wg) * (x @ wu)) @ wd


def route(x2, w_router, router_bias):
    t = x2.shape[0]
    scores = jax.nn.sigmoid((x2 @ w_router).astype(jnp.float32))
    biased = scores + router_bias.astype(jnp.float32)
    grp = biased.reshape(t, N_GROUPS, N_EXPERTS // N_GROUPS)
    grp_score = jnp.sum(lax.top_k(grp, 2)[0], axis=-1)
    _, gidx = lax.top_k(grp_score, TOPK_GROUPS)
    gmask = jnp.sum(jax.nn.one_hot(gidx, N_GROUPS), axis=1) > 0
    emask = jnp.repeat(gmask, N_EXPERTS // N_GROUPS, axis=1)
    _, eidx = lax.top_k(jnp.where(emask, biased, -jnp.inf), TOP_K)
    gw = jnp.take_along_axis(scores, eidx, axis=-1)
    gw = gw / jnp.sum(gw, axis=-1, keepdims=True) * ROUTED_SCALE
    return eidx, gw


def routed_experts(x2, eidx, gw, w_e_gate, w_e_up, w_e_down):
    t, d = x2.shape
    a = t * TOP_K
    flat_e = eidx.reshape(-1)
    flat_tok = jnp.arange(a, dtype=jnp.int32) // TOP_K
    flat_w = gw.reshape(-1)
    order = jnp.argsort(flat_e)
    sorted_e = flat_e[order]
    counts = jnp.bincount(flat_e, length=N_EXPERTS)
    starts = jnp.cumsum(counts) - counts
    padded = (counts + MOE_BLOCK - 1) // MOE_BLOCK * MOE_BLOCK
    pad_end = jnp.cumsum(padded)
    pad_start = pad_end - padded
    dest = pad_start[sorted_e] + jnp.arange(a) - starts[sorted_e]
    n_blocks = -(-(a + N_EXPERTS * (MOE_BLOCK - 1)) // MOE_BLOCK)
    rows = n_blocks * MOE_BLOCK
    row_tok = jnp.full((rows,), t, jnp.int32).at[dest].set(flat_tok[order])
    row_w = jnp.zeros((rows,), jnp.float32).at[dest].set(flat_w[order])
    block_e = jnp.minimum(
        jnp.searchsorted(pad_end, jnp.arange(n_blocks) * MOE_BLOCK, side='right'),
        N_EXPERTS - 1)
    x_pad = jnp.concatenate([x2, jnp.zeros((1, d), x2.dtype)], axis=0)
    xb = x_pad[row_tok].reshape(n_blocks, MOE_BLOCK, d)

    def expert_block(args):
        xblk, e = args
        return swiglu(xblk, w_e_gate[e], w_e_up[e], w_e_down[e])

    yb = lax.map(expert_block, (xb, block_e)).reshape(rows, d)
    y = jax.ops.segment_sum(yb * row_w[:, None].astype(yb.dtype), row_tok, num_segments=t + 1)
    return y[:t].astype(x2.dtype)


def decoder_layer(h, p, attn_fn, conv_prev, lw):
    n, s, _ = h.shape
    z = h @ lw['w_in']
    q, k, v, c_b, c_c, c_h, g_a, g_c = jnp.split(z, IN_SPLITS, axis=-1)
    q = q.reshape(n, s, N_HEADS, HEAD_DIM)
    k = k.reshape(n, s, N_KV_HEADS, HEAD_DIM)
    v = v.reshape(n, s, N_KV_HEADS, HEAD_DIM)
    attn, new_k, new_v = attn_fn(q, k, v, lw['attn_sinks'])
    conv, new_conv = short_conv(c_c * c_h, conv_prev, lw['conv_w'])
    conv = c_b * conv
    merged = (jax.nn.sigmoid(g_a) * (attn @ lw['w_attn_out'])
              + jax.nn.sigmoid(g_c) * (conv @ lw['w_conv_out']))
    mix = merged @ lw['w_o']
    h = layer_norm(DN_ALPHA * h + mix, lw['ln1_g'], lw['ln1_b'])
    x2 = h.reshape(n * s, D_MODEL)
    eidx, gw = route(x2, lw['w_router'], lw['router_bias'])
    ffn = (swiglu(x2, lw['w_s_gate'], lw['w_s_up'], lw['w_s_down'])
           + routed_experts(x2, eidx, gw, lw['w_e_gate'], lw['w_e_up'], lw['w_e_down']))
    ple = jax.nn.sigmoid(h @ lw['w_ple_gate']) * (p @ lw['w_ple_proj'])
    h = layer_norm(DN_ALPHA * h + ffn.reshape(n, s, D_MODEL) + ple, lw['ln2_g'], lw['ln2_b'])
    return h, new_k, new_v, new_conv


def setup_inputs(seed: int = 0) -> dict:
    key = jax.random.key(seed)
    ks = jax.random.split(key, 32)

    def nrm(k, shape, scale):
        return jax.random.normal(k, shape, jnp.float32) * scale

    wc = min(WINDOW, PAST_LEN)
    return {
        'x_prompt': nrm(ks[0], (BATCH, SEQ, D_MODEL), 1.0),
        'x_sample': nrm(ks[1], (DEC_BATCH, DEC_SEQ, D_MODEL), 1.0),
        'p_prompt': nrm(ks[2], (DEPTH, BATCH, SEQ, PLE_DIM), 1.0),
        'p_sample': nrm(ks[3], (DEPTH, DEC_BATCH, DEC_SEQ, PLE_DIM), 1.0),
        'state_win_k': nrm(ks[4], (DEPTH, DEC_BATCH, wc, N_KV_HEADS, HEAD_DIM), 1.0),
        'state_win_v': nrm(ks[5], (DEPTH, DEC_BATCH, wc, N_KV_HEADS, HEAD_DIM), 1.0),
        'state_conv': nrm(ks[6], (DEPTH, DEC_BATCH, CONV_WIDTH - 1, CONV_DIM), 1.0),
        'w_in': nrm(ks[7], (DEPTH, D_MODEL, IN_COLS), D_MODEL ** -0.5),
        'attn_sinks': nrm(ks[8], (DEPTH, N_HEADS), 0.5),
        'conv_w': nrm(ks[9], (DEPTH, CONV_WIDTH, CONV_DIM), CONV_WIDTH ** -0.5),
        'w_attn_out': nrm(ks[10], (DEPTH, Q_W, D_MODEL), Q_W ** -0.5),
        'w_conv_out': nrm(ks[11], (DEPTH, CONV_DIM, D_MODEL), CONV_DIM ** -0.5),
        'w_o': nrm(ks[12], (DEPTH, D_MODEL, D_MODEL), D_MODEL ** -0.5 * DN_BETA),
        'ln1_g': 1.0 + nrm(ks[13], (DEPTH, D_MODEL), 0.01),
        'ln1_b': nrm(ks[14], (DEPTH, D_MODEL), 0.01),
        'w_router': nrm(ks[15], (DEPTH, D_MODEL, N_EXPERTS), D_MODEL ** -0.5),
        'router_bias': nrm(ks[16], (DEPTH, N_EXPERTS), 0.01),
        'w_e_gate': nrm(ks[17], (DEPTH, N_EXPERTS, D_MODEL, EXPERT_FF), D_MODEL ** -0.5),
        'w_e_up': nrm(ks[18], (DEPTH, N_EXPERTS, D_MODEL, EXPERT_FF), D_MODEL ** -0.5),
        'w_e_down': nrm(ks[19], (DEPTH, N_EXPERTS, EXPERT_FF, D_MODEL), EXPERT_FF ** -0.5 * DN_BETA),
        'w_s_gate': nrm(ks[20], (DEPTH, D_MODEL, SHARED_FF), D_MODEL ** -0.5),
        'w_s_up': nrm(ks[21], (DEPTH, D_MODEL, SHARED_FF), D_MODEL ** -0.5),
        'w_s_down': nrm(ks[22], (DEPTH, SHARED_FF, D_MODEL), SHARED_FF ** -0.5 * DN_BETA),
        'w_ple_gate': nrm(ks[23], (DEPTH, D_MODEL, D_MODEL), D_MODEL ** -0.5),
        'w_ple_proj': nrm(ks[24], (DEPTH, PLE_DIM, D_MODEL), PLE_DIM ** -0.5),
        'ln2_g': 1.0 + nrm(ks[25], (DEPTH, D_MODEL), 0.01),
        'ln2_b': nrm(ks[26], (DEPTH, D_MODEL), 0.01),
    }


def reference(x_prompt, x_sample, p_prompt, p_sample, state_win_k, state_win_v, state_conv,
              w_in, attn_sinks, conv_w, w_attn_out, w_conv_out, w_o, ln1_g, ln1_b,
              w_router, router_bias, w_e_gate, w_e_up, w_e_down, w_s_gate, w_s_up, w_s_down,
              w_ple_gate, w_ple_proj, ln2_g, ln2_b):
    hp, hs = x_prompt, x_sample
    kp_l, vp_l, cp_l, ks_l, vs_l, cs_l = [], [], [], [], [], []
    for i in range(DEPTH):
        lw = {
            'w_in': w_in[i], 'attn_sinks': attn_sinks[i], 'conv_w': conv_w[i],
            'w_attn_out': w_attn_out[i], 'w_conv_out': w_conv_out[i], 'w_o': w_o[i],
            'ln1_g': ln1_g[i], 'ln1_b': ln1_b[i], 'w_router': w_router[i],
            'router_bias': router_bias[i], 'w_e_gate': w_e_gate[i], 'w_e_up': w_e_up[i],
            'w_e_down': w_e_down[i], 'w_s_gate': w_s_gate[i], 'w_s_up': w_s_up[i],
            'w_s_down': w_s_down[i], 'w_ple_gate': w_ple_gate[i], 'w_ple_proj': w_ple_proj[i],
            'ln2_g': ln2_g[i], 'ln2_b': ln2_b[i],
        }
        conv0 = jnp.zeros((hp.shape[0], CONV_WIDTH - 1, CONV_DIM), hp.dtype)
        hp, kp, vp, cp = decoder_layer(hp, p_prompt[i], swa_prompt, conv0, lw)
        attn_s = functools.partial(swa_sample, cache_k=state_win_k[i], cache_v=state_win_v[i])
        hs, ks_, vs_, cs_ = decoder_layer(hs, p_sample[i], attn_s, state_conv[i], lw)
        kp_l.append(kp); vp_l.append(vp); cp_l.append(cp)
        ks_l.append(ks_); vs_l.append(vs_); cs_l.append(cs_)
    return (hp, hs, jnp.stack(kp_l), jnp.stack(vp_l), jnp.stack(cp_l),
            jnp.stack(ks_l), jnp.stack(vs_l), jnp.stack(cs_l))
```

```python
import functools

import jax
import jax.numpy as jnp
import numpy as np
from jax import lax
from jax.experimental import pallas as pl
from jax.experimental.pallas import tpu as pltpu

F32 = jnp.float32
BF16 = jnp.bfloat16
I32 = jnp.int32

D = 1024
BATCH, SEQ = 4, 4096
DEC_BATCH, DEC_SEQ = 128, 8
N_HEADS, N_KV, HD = 16, 2, 64
WINDOW = 128
KVW = N_KV * HD
CONV_W = 3
N_EXP, TOP_K, N_GRP, TOPK_GRP = 64, 6, 8, 4
EXP_PER_GRP = N_EXP // N_GRP
EFF = 256
SFF = 256
PLE = 256
ROUTED_SCALE = 2.5
LN_EPS = 1e-5
DN_ALPHA = 2.0 ** 0.25
T_P = BATCH * SEQ
T_S = DEC_BATCH * DEC_SEQ
T = T_P + T_S

C_Q, C_K, C_V = 0, 1024, 1152
C_CB, C_CC, C_CH, C_GA, C_GC = 1280, 2304, 3328, 4352, 5376

TQ = 512
SEQ_PER_STEP = 16
TS = 256
N_TILES = T // TS
N_TILES_P = T_P // TS
CH = 16
R_LOC = 2560
PCH = 256
N_CHUNK = R_LOC // CH
RB = 512
R_TOT = ((N_TILES * (TS * TOP_K + N_EXP * (CH - 1)) + N_EXP * (RB - 1)) // RB + 1) * RB
N_BLK = R_TOT // RB
NEG = -jnp.inf

VMEM_LIMIT = 56 * 1024 * 1024


def _dot(a, b):
    return jnp.dot(a, b, preferred_element_type=F32)


def _dot_nt(a, b):
    return lax.dot_general(a, b, (((1,), (1,)), ((), ())), preferred_element_type=F32)


def _dot_tn(a, b):
    return lax.dot_general(a, b, (((0,), (0,)), ((), ())), preferred_element_type=F32)


def _sigmoid(x):
    return 1.0 / (1.0 + jnp.exp(-x))


def _layer_norm(r, g, b):
    mu = jnp.mean(r, axis=-1, keepdims=True)
    rc = r - mu
    var = jnp.mean(rc * rc, axis=-1, keepdims=True)
    return rc * lax.rsqrt(var + LN_EPS) * g + b


def _resident(shape):
    nd = len(shape)
    return pl.BlockSpec(shape, lambda *_: (0,) * nd, pipeline_mode=pl.Buffered(1))


def _softmax_sink(l, valid, sink):
    l = jnp.where(valid, l, NEG)
    m = jnp.maximum(jnp.max(l, axis=-1, keepdims=True), sink)
    e = jnp.where(valid, jnp.exp(l - m), 0.0)
    return e / (jnp.sum(e, axis=-1, keepdims=True) + jnp.exp(sink - m))


def _conv_taps(u_prev2, u_prev1, u, cw):
    return u_prev2 * cw[0:1, :] + u_prev1 * cw[1:2, :] + u * cw[2:3, :]


def _mixer_tail(x, xb, attn_b, y, win_ref, wa_ref, wc_ref, wo_ref, g_ref, b_ref):
    def proj(c):
        return _dot(xb, win_ref[:, c:c + D])

    merged = _sigmoid(proj(C_GA)) * _dot(attn_b, wa_ref[...])
    conv = proj(C_CB) * y
    merged = merged + _sigmoid(proj(C_GC)) * _dot(conv.astype(BF16), wc_ref[...])
    mix = _dot(merged.astype(BF16), wo_ref[...])
    return _layer_norm(DN_ALPHA * x + mix, g_ref[...], b_ref[...])


def _mixer_prompt_kernel(x_ref, win_ref, sink_ref, cw_ref, wa_ref, wc_ref, wo_ref, g_ref, b_ref,
                         h_ref, kwin_ref, vwin_ref, cst_ref,
                         kbuf, vbuf, ubuf, q_scr, attn_scr):
    i = pl.program_id(1)

    @pl.when(i == 0)
    def _():
        kbuf[0:WINDOW, :] = jnp.zeros((WINDOW, KVW), F32)
        vbuf[0:WINDOW, :] = jnp.zeros((WINDOW, KVW), F32)
        ubuf[0:8, :] = jnp.zeros((8, D), F32)

    x = x_ref[0]
    xb = x.astype(BF16)
    q_scr[...] = (_dot(xb, win_ref[:, C_Q:C_Q + D]) * (HD ** -0.5)).astype(BF16)
    kv = _dot(xb, win_ref[:, C_K:C_K + 2 * KVW])
    kbuf[WINDOW:WINDOW + TQ, :] = kv[:, :KVW]
    vbuf[WINDOW:WINDOW + TQ, :] = kv[:, KVW:]

    lane = lax.broadcasted_iota(I32, (WINDOW + TQ, KVW), 1)
    kall, vall = kbuf[...], vbuf[...]
    k_lo = jnp.where(lane < HD, kall, 0.0).astype(BF16)
    k_hi = jnp.where(lane >= HD, kall, 0.0).astype(BF16)
    v_lo = jnp.where(lane < HD, vall, 0.0).astype(BF16)
    v_hi = jnp.where(lane >= HD, vall, 0.0).astype(BF16)

    rows = 8 * WINDOW
    rq = lax.broadcasted_iota(I32, (rows, 2 * WINDOW), 0) & (WINDOW - 1)
    sk = lax.broadcasted_iota(I32, (rows, 2 * WINDOW), 1)
    band = (sk > rq) & (sk <= rq + WINDOW)
    sink0 = sink_ref[0]
    sink1 = sink_ref[1]

    for j in range(TQ // WINDOW):
        r0 = j * WINDOW
        qs = jnp.concatenate([q_scr[r0:r0 + WINDOW, p * 128:(p + 1) * 128] for p in range(8)], axis=0)
        kcat = jnp.concatenate([k_lo[r0:r0 + 2 * WINDOW], k_hi[r0:r0 + 2 * WINDOW]], axis=0)
        vcat = jnp.concatenate([v_lo[r0:r0 + 2 * WINDOW], v_hi[r0:r0 + 2 * WINDOW]], axis=0)
        logits = _dot_nt(qs, kcat)
        if j == 0:
            valid = band & (sk >= jnp.where(i > 0, 0, WINDOW))
        else:
            valid = band
        p0 = _softmax_sink(logits[:, :2 * WINDOW], valid, sink0)
        p1 = _softmax_sink(logits[:, 2 * WINDOW:], valid, sink1)
        pcat = jnp.concatenate([p0, p1], axis=1).astype(BF16)
        o = _dot(pcat, vcat)
        for p in range(8):
            attn_scr[r0:r0 + WINDOW, p * 128:(p + 1) * 128] = o[p * WINDOW:(p + 1) * WINDOW].astype(BF16)

    u = _dot(xb, win_ref[:, C_CC:C_CC + D]) * _dot(xb, win_ref[:, C_CH:C_CH + D])
    ubuf[8:8 + TQ, :] = u
    y = _conv_taps(ubuf[6:6 + TQ, :], ubuf[7:7 + TQ, :], u, cw_ref[...])
    h_ref[...] = _mixer_tail(x, xb, attn_scr[...], y, win_ref, wa_ref, wc_ref, wo_ref, g_ref, b_ref)

    kwin_ref[0] = kbuf[TQ:TQ + WINDOW, :]
    vwin_ref[0] = vbuf[TQ:TQ + WINDOW, :]
    cst_ref[0] = ubuf[TQ + 6:TQ + 8, :]
    kbuf[0:WINDOW, :] = kbuf[TQ:TQ + WINDOW, :]
    vbuf[0:WINDOW, :] = vbuf[TQ:TQ + WINDOW, :]
    ubuf[0:8, :] = ubuf[TQ:TQ + 8, :]


def _mixer_prompt(x, win, sink_rows, cw, wa, wc, wo, g, b):
    nseq = SEQ // TQ
    return pl.pallas_call(
        _mixer_prompt_kernel,
        grid=(BATCH, nseq),
        in_specs=[
            pl.BlockSpec((1, TQ, D), lambda n, i: (n, i, 0)),
            _resident(win.shape), _resident(sink_rows.shape), _resident(cw.shape),
            _resident(wa.shape), _resident(wc.shape), _resident(wo.shape),
            _resident(g.shape), _resident(b.shape),
        ],
        out_specs=[
            pl.BlockSpec((TQ, D), lambda n, i: (n * nseq + i, 0)),
            pl.BlockSpec((1, WINDOW, KVW), lambda n, i: (n, 0, 0)),
            pl.BlockSpec((1, WINDOW, KVW), lambda n, i: (n, 0, 0)),
            pl.BlockSpec((1, CONV_W - 1, D), lambda n, i: (n, 0, 0)),
        ],
        out_shape=[
            jax.ShapeDtypeStruct((T, D), F32),
            jax.ShapeDtypeStruct((BATCH, WINDOW, KVW), F32),
            jax.ShapeDtypeStruct((BATCH, WINDOW, KVW), F32),
            jax.ShapeDtypeStruct((BATCH, CONV_W - 1, D), F32),
        ],
        scratch_shapes=[
            pltpu.VMEM((WINDOW + TQ, KVW), F32),
            pltpu.VMEM((WINDOW + TQ, KVW), F32),
            pltpu.VMEM((8 + TQ, D), F32),
            pltpu.VMEM((TQ, D), BF16),
            pltpu.VMEM((TQ, D), BF16),
        ],
        compiler_params=pltpu.CompilerParams(
            dimension_semantics=("arbitrary", "arbitrary"), vmem_limit_bytes=VMEM_LIMIT),
        name="mixer_prompt",
    )(x, win, sink_rows, cw, wa, wc, wo, g, b)


def _mixer_sample_kernel(h_any, x_ref, ck_ref, cv_ref, cs_ref, win_ref, sink_ref, cw_ref, wa_ref, wc_ref,
                         wo_ref, g_ref, b_ref,
                         h_ref, kwin_ref, vwin_ref, cst_ref,
                         q_scr, attn_scr, u_scr, y_scr):
    del h_any
    G, S = SEQ_PER_STEP, DEC_SEQ
    n = G * S
    x = x_ref[...].reshape(n, D)
    xb = x.astype(BF16)
    q_scr[...] = _dot(xb, win_ref[:, C_Q:C_Q + D]) * (HD ** -0.5)
    kv = _dot(xb, win_ref[:, C_K:C_K + 2 * KVW])
    k_new, v_new = kv[:, :KVW], kv[:, KVW:]
    u_scr[...] = _dot(xb, win_ref[:, C_CC:C_CC + D]) * _dot(xb, win_ref[:, C_CH:C_CH + D])
    cw = cw_ref[...]

    lane_c = lax.broadcasted_iota(I32, (WINDOW, KVW), 1)
    lane_n = lax.broadcasted_iota(I32, (S, KVW), 1)
    rows = 8 * S
    sq = lax.broadcasted_iota(I32, (rows, WINDOW), 0) & (S - 1)
    cc = lax.broadcasted_iota(I32, (rows, WINDOW), 1)
    valid_c = cc > sq
    sqn = lax.broadcasted_iota(I32, (rows, 2 * S), 0) & (S - 1)
    cn = lax.broadcasted_iota(I32, (rows, 2 * S), 1)
    valid_n0 = (cn < S) & (cn <= sqn)
    valid_n1 = (cn >= S) & (cn - S <= sqn)
    sink0 = sink_ref[0]
    sink1 = sink_ref[1]
    srow = lax.broadcasted_iota(I32, (S, D), 0)

    def softmax2(lc, ln, vn, sink):
        lc = jnp.where(valid_c, lc, NEG)
        ln = jnp.where(vn, ln, NEG)
        m = jnp.maximum(jnp.maximum(jnp.max(lc, axis=-1, keepdims=True),
                                    jnp.max(ln, axis=-1, keepdims=True)), sink)
        ec = jnp.where(valid_c, jnp.exp(lc - m), 0.0)
        en = jnp.where(vn, jnp.exp(ln - m), 0.0)
        den = jnp.sum(ec, axis=-1, keepdims=True) + jnp.sum(en, axis=-1, keepdims=True) + jnp.exp(sink - m)
        return ec / den, en / den

    def lo_hi(a, lane):
        return jnp.concatenate([jnp.where(lane < HD, a, 0.0), jnp.where(lane >= HD, a, 0.0)],
                               axis=0).astype(BF16)

    for s in range(G):
        t0 = s * S
        kc, vc = ck_ref[s], cv_ref[s]
        kn, vn = k_new[t0:t0 + S], v_new[t0:t0 + S]
        qs = jnp.concatenate([q_scr[t0:t0 + S, p * 128:(p + 1) * 128] for p in range(8)],
                             axis=0).astype(BF16)
        lc = _dot_nt(qs, lo_hi(kc, lane_c))
        ln = _dot_nt(qs, lo_hi(kn, lane_n))
        pc0, pn0 = softmax2(lc[:, :WINDOW], ln, valid_n0, sink0)
        pc1, pn1 = softmax2(lc[:, WINDOW:], ln, valid_n1, sink1)
        o = (_dot(jnp.concatenate([pc0, pc1], axis=1).astype(BF16), lo_hi(vc, lane_c))
             + _dot((pn0 + pn1).astype(BF16), lo_hi(vn, lane_n)))
        for p in range(8):
            attn_scr[t0:t0 + S, p * 128:(p + 1) * 128] = o[p * S:(p + 1) * S]
        kwin_ref[s, 0:WINDOW - S, :] = kc[S:, :]
        kwin_ref[s, WINDOW - S:WINDOW, :] = kn
        vwin_ref[s, 0:WINDOW - S, :] = vc[S:, :]
        vwin_ref[s, WINDOW - S:WINDOW, :] = vn

        us = u_scr[t0:t0 + S, :]
        prev0, prev1 = cs_ref[s, 0:1, :], cs_ref[s, 1:2, :]
        r1 = pltpu.roll(us, 1, axis=0)
        r2 = pltpu.roll(us, 2, axis=0)
        cst_ref[s] = r2[0:CONV_W - 1, :]
        um1 = jnp.where(srow == 0, prev1, r1)
        um2 = jnp.where(srow == 0, prev0, jnp.where(srow == 1, prev1, r2))
        y_scr[t0:t0 + S, :] = _conv_taps(um2, um1, us, cw)

    h_ref[...] = _mixer_tail(x, xb, attn_scr[...].astype(BF16), y_scr[...],
                             win_ref, wa_ref, wc_ref, wo_ref, g_ref, b_ref)


def _mixer_sample(h_all, x, ck, cv, cs, win, sink_rows, cw, wa, wc, wo, g, b):
    G = SEQ_PER_STEP
    n = G * DEC_SEQ
    base_blk = T_P // n
    return pl.pallas_call(
        _mixer_sample_kernel,
        grid=(DEC_BATCH // G,),
        in_specs=[
            pl.BlockSpec(memory_space=pl.ANY),
            pl.BlockSpec((G, DEC_SEQ, D), lambda i: (i, 0, 0)),
            pl.BlockSpec((G, WINDOW, KVW), lambda i: (i, 0, 0)),
            pl.BlockSpec((G, WINDOW, KVW), lambda i: (i, 0, 0)),
            pl.BlockSpec((G, CONV_W - 1, D), lambda i: (i, 0, 0)),
            _resident(win.shape), _resident(sink_rows.shape), _resident(cw.shape),
            _resident(wa.shape), _resident(wc.shape), _resident(wo.shape),
            _resident(g.shape), _resident(b.shape),
        ],
        out_specs=[
            pl.BlockSpec((n, D), lambda i: (base_blk + i, 0)),
            pl.BlockSpec((G, WINDOW, KVW), lambda i: (i, 0, 0)),
            pl.BlockSpec((G, WINDOW, KVW), lambda i: (i, 0, 0)),
            pl.BlockSpec((G, CONV_W - 1, D), lambda i: (i, 0, 0)),
        ],
        out_shape=[
            jax.ShapeDtypeStruct((T, D), F32),
            jax.ShapeDtypeStruct((DEC_BATCH, WINDOW, KVW), F32),
            jax.ShapeDtypeStruct((DEC_BATCH, WINDOW, KVW), F32),
            jax.ShapeDtypeStruct((DEC_BATCH, CONV_W - 1, D), F32),
        ],
        scratch_shapes=[pltpu.VMEM((n, D), F32)] * 4,
        input_output_aliases={0: 0},
        compiler_params=pltpu.CompilerParams(
            dimension_semantics=("arbitrary",), vmem_limit_bytes=VMEM_LIMIT),
        name="mixer_sample",
    )(h_all, x, ck, cv, cs, win, sink_rows, cw, wa, wc, wo, g, b)


def _first_index_of_max(v, idx, big):
    m = v[0]
    for a in v[1:]:
        m = jnp.maximum(m, a)
    m = jnp.max(m, axis=0, keepdims=True)
    f = None
    for a, ix in zip(v, idx):
        c = jnp.where(a == m, ix, big)
        f = c if f is None else jnp.minimum(f, c)
    return m, jnp.min(f, axis=0, keepdims=True)


def _router_kernel(h_ref, wr_ref, rb_ref, eidx_ref, gw_ref, rank_ref, cnt_ref):
    hb = h_ref[...].astype(BF16)
    scores = _sigmoid(_dot_nt(wr_ref[...], hb))
    biased = scores + rb_ref[...]
    sub = lax.broadcasted_iota(I32, (EXP_PER_GRP, TS), 0)
    sc = [scores[g * 8:(g + 1) * 8] for g in range(N_GRP)]
    bi = [biased[g * 8:(g + 1) * 8] for g in range(N_GRP)]
    eid = [sub + g * 8 for g in range(N_GRP)]

    gs = []
    for g in range(N_GRP):
        m1 = jnp.max(bi[g], axis=0, keepdims=True)
        first = jnp.min(jnp.where(bi[g] == m1, sub, 8), axis=0, keepdims=True)
        m2 = jnp.max(jnp.where(sub == first, NEG, bi[g]), axis=0, keepdims=True)
        gs.append(m1 + m2)
    gscore = jnp.concatenate(gs, axis=0)
    gsel = jnp.zeros((N_GRP, TS), F32)
    for _ in range(TOPK_GRP):
        m = jnp.max(gscore, axis=0, keepdims=True)
        first = jnp.min(jnp.where(gscore == m, sub, 8), axis=0, keepdims=True)
        hit = sub == first
        gsel = jnp.where(hit, 1.0, gsel)
        gscore = jnp.where(hit, NEG, gscore)
    cand = [jnp.where(gsel[g:g + 1, :] > 0.5, bi[g], NEG) for g in range(N_GRP)]

    e_rows, w_rows, sel_any = [], [], [jnp.zeros((8, TS), F32)] * N_GRP
    for _ in range(TOP_K):
        _, first = _first_index_of_max(cand, eid, N_EXP)
        hits = [eid[g] == first for g in range(N_GRP)]
        w = None
        for g in range(N_GRP):
            c = jnp.sum(jnp.where(hits[g], sc[g], 0.0), axis=0, keepdims=True)
            w = c if w is None else w + c
        e_rows.append(first)
        w_rows.append(w)
        cand = [jnp.where(hits[g], NEG, cand[g]) for g in range(N_GRP)]
        sel_any = [jnp.where(hits[g], 1.0, sel_any[g]) for g in range(N_GRP)]

    wsum = w_rows[0]
    for w in w_rows[1:]:
        wsum = wsum + w
    zero_i = jnp.zeros((1, TS), I32)
    zero_f = jnp.zeros((1, TS), F32)
    eidx_ref[0] = jnp.concatenate(e_rows + [zero_i, zero_i], axis=0)
    gw_ref[0] = jnp.concatenate([w / wsum * ROUTED_SCALE for w in w_rows] + [zero_f, zero_f], axis=0)

    sel = jnp.concatenate(sel_any, axis=0)
    tr = lax.broadcasted_iota(I32, (TS, TS), 0)
    tc = lax.broadcasted_iota(I32, (TS, TS), 1)
    upper = jnp.where(tr < tc, 1.0, 0.0).astype(BF16)
    prefix = _dot(sel.astype(BF16), upper)
    e_all = jnp.concatenate(eid, axis=0)
    ranks = []
    for k in range(TOP_K):
        ranks.append(jnp.sum(jnp.where(e_all == e_rows[k], prefix, 0.0), axis=0, keepdims=True))
    rank_ref[0] = jnp.concatenate(ranks + [zero_f, zero_f], axis=0).astype(I32)
    cnt_ref[0] = _dot_nt(jnp.ones((8, TS), BF16), sel.astype(BF16))


def _router(h_all, wr_t, rb):
    blk = lambda: pl.BlockSpec((1, 8, TS), lambda i: (i, 0, 0))
    return pl.pallas_call(
        _router_kernel,
        grid=(N_TILES,),
        in_specs=[pl.BlockSpec((TS, D), lambda i: (i, 0)), _resident(wr_t.shape), _resident(rb.shape)],
        out_specs=[blk(), blk(), blk(), pl.BlockSpec((1, 8, N_EXP), lambda i: (i, 0, 0))],
        out_shape=[
            jax.ShapeDtypeStruct((N_TILES, 8, TS), I32),
            jax.ShapeDtypeStruct((N_TILES, 8, TS), F32),
            jax.ShapeDtypeStruct((N_TILES, 8, TS), I32),
            jax.ShapeDtypeStruct((N_TILES, 8, N_EXP), F32),
        ],
        compiler_params=pltpu.CompilerParams(dimension_semantics=("arbitrary",)),
        name="router",
    )(h_all, wr_t, rb)


def _excl_cumsum(a, axis):
    return jnp.cumsum(a, axis=axis) - a


def _dispatch_tables(cnt):
    pc = (cnt + CH - 1) // CH * CH
    loc = _excl_cumsum(pc, 1)
    nch = jnp.sum(pc, axis=1) // CH
    tot = jnp.sum(pc, axis=0)
    reg = (tot + RB - 1) // RB * RB
    rstart = _excl_cumsum(reg, 0)
    glob = rstart[None, :] + _excl_cumsum(pc, 0)
    q0 = jnp.arange(N_CHUNK, dtype=I32) * CH
    e_q = jnp.sum((loc + pc)[:, None, :] <= q0[None, :, None], axis=2).astype(I32)
    e_q = jnp.minimum(e_q, N_EXP - 1)
    dst = (jnp.take_along_axis(glob, e_q, axis=1) + q0[None, :] - jnp.take_along_axis(loc, e_q, axis=1))
    nblk_e = reg // RB
    bend = jnp.cumsum(nblk_e)
    nblk = bend[-1]
    b = jnp.minimum(jnp.arange(N_BLK, dtype=I32), nblk - 1)
    block_e = jnp.minimum(jnp.sum(bend[None, :] <= b[:, None], axis=1), N_EXP - 1).astype(I32)
    tail0 = rstart + tot
    tail_n = reg - tot
    return dict(loc=loc.astype(I32), nch=nch.astype(I32), dst=dst.astype(I32).reshape(-1),
                block_e=block_e, nblk=nblk.astype(I32).reshape(1),
                tail0=tail0.astype(I32), tail_n=tail_n.astype(I32))


def _local_dest(eidx, rank, loc_col):
    e_iota = lax.broadcasted_iota(I32, (N_EXP, TS), 0)
    out = []
    for k in range(TOP_K):
        off = jnp.sum(jnp.where(e_iota == eidx[k:k + 1, :], loc_col, 0.0), axis=0, keepdims=True)
        out.append(off.astype(I32) + rank[k:k + 1, :])
    return out


TAIL_SIZES = (256, 128, 64, 32, 16)


def _shared_dispatch_kernel(nch_ref, dst_ref, tail0_ref, tailn_ref,
                            h_ref, pp_ref, ps_ref, eidx_ref, rank_ref, loc_ref,
                            wsgu_ref, wsd_ref, wpg_ref, wpp_ref,
                            base_ref, xs_any,
                            sorted_scr, zero_scr, sem, tsem):
    i = pl.program_id(0)
    h = h_ref[...]
    hb = h.astype(BF16)
    nch = nch_ref[i]

    ldest = _local_dest(eidx_ref[0], rank_ref[0], loc_ref[0])
    n_p = (nch * CH + PCH - 1) // PCH

    def perm_chunk(c, carry):
        r0 = pl.multiple_of(c * PCH, PCH)
        rid = lax.broadcasted_iota(I32, (PCH, TS), 0) + r0
        hit = rid == ldest[0]
        for k in range(1, TOP_K):
            hit = hit | (rid == ldest[k])
        onehot = jnp.where(hit, 1.0, 0.0).astype(BF16)
        sorted_scr[pl.ds(r0, PCH), :] = _dot(onehot, hb).astype(BF16)
        return carry

    lax.fori_loop(0, n_p, perm_chunk, 0)

    def chunk_copy(q):
        src = pl.multiple_of(q * CH, CH)
        d = pl.multiple_of(dst_ref[i * N_CHUNK + q], CH)
        return pltpu.make_async_copy(sorted_scr.at[pl.ds(src, CH), :], xs_any.at[pl.ds(d, CH), :], sem)

    def start(q, carry):
        chunk_copy(q).start()
        return carry

    lax.fori_loop(0, nch, start, 0)

    def tail_copy(size, row):
        return pltpu.make_async_copy(zero_scr.at[pl.ds(0, size), :],
                                     xs_any.at[pl.ds(pl.multiple_of(row, CH), size), :], tsem)

    @pl.when(i == 0)
    def _():
        zero_scr[...] = jnp.zeros(zero_scr.shape, BF16)

        def fill(e, carry):
            row = tail0_ref[e]
            left = tailn_ref[e]
            for size in TAIL_SIZES:
                @pl.when((left & size) != 0)
                def _():
                    tail_copy(size, row).start()
                row = row + (left & size)
            return carry

        lax.fori_loop(0, N_EXP, fill, 0)

    gu = _dot(hb, wsgu_ref[...])
    g, u = gu[:, :SFF], gu[:, SFF:]
    shared = _dot((g * _sigmoid(g) * u).astype(BF16), wsd_ref[...])
    p = jnp.where(i < N_TILES_P, pp_ref[...], ps_ref[...]).astype(BF16)
    ple = _sigmoid(_dot(hb, wpg_ref[...])) * _dot(p, wpp_ref[...])
    base_ref[...] = DN_ALPHA * h + shared + ple

    def wait(q, carry):
        chunk_copy(q).wait()
        return carry

    lax.fori_loop(0, nch, wait, 0)

    @pl.when(i == 0)
    def _():
        def drain(e, carry):
            left = tailn_ref[e]
            for size in TAIL_SIZES:
                @pl.when((left & size) != 0)
                def _():
                    tail_copy(size, 0).wait()
            return carry

        lax.fori_loop(0, N_EXP, drain, 0)


def _shared_dispatch(tb, h_all, pp, ps, eidx, rank, loc3, wsgu, wsd, wpg, wpp):
    grid_spec = pltpu.PrefetchScalarGridSpec(
        num_scalar_prefetch=4,
        grid=(N_TILES,),
        in_specs=[
            pl.BlockSpec((TS, D), lambda i, *_: (i, 0)),
            pl.BlockSpec((TS, PLE), lambda i, *_: (jnp.minimum(i, N_TILES_P - 1), 0)),
            pl.BlockSpec((TS, PLE), lambda i, *_: (jnp.maximum(i - N_TILES_P, 0), 0)),
            pl.BlockSpec((1, 8, TS), lambda i, *_: (i, 0, 0)),
            pl.BlockSpec((1, 8, TS), lambda i, *_: (i, 0, 0)),
            pl.BlockSpec((1, N_EXP, 1), lambda i, *_: (i, 0, 0)),
            pl.BlockSpec(wsgu.shape, lambda i, *_: (0, 0), pipeline_mode=pl.Buffered(1)),
            pl.BlockSpec(wsd.shape, lambda i, *_: (0, 0), pipeline_mode=pl.Buffered(1)),
            pl.BlockSpec(wpg.shape, lambda i, *_: (0, 0), pipeline_mode=pl.Buffered(1)),
            pl.BlockSpec(wpp.shape, lambda i, *_: (0, 0), pipeline_mode=pl.Buffered(1)),
        ],
        out_specs=[
            pl.BlockSpec((TS, D), lambda i, *_: (i, 0)),
            pl.BlockSpec(memory_space=pl.ANY),
        ],
        scratch_shapes=[
            pltpu.VMEM((R_LOC, D), BF16),
            pltpu.VMEM((TAIL_SIZES[0], D), BF16),
            pltpu.SemaphoreType.DMA(()),
            pltpu.SemaphoreType.DMA(()),
        ],
    )
    return pl.pallas_call(
        _shared_dispatch_kernel,
        grid_spec=grid_spec,
        out_shape=[jax.ShapeDtypeStruct((T, D), F32), jax.ShapeDtypeStruct((R_TOT, D), BF16)],
        compiler_params=pltpu.CompilerParams(
            dimension_semantics=("arbitrary",), vmem_limit_bytes=VMEM_LIMIT),
        name="shared_dispatch",
    )(tb["nch"], tb["dst"], tb["tail0"], tb["tail_n"],
      h_all, pp, ps, eidx, rank, loc3, wsgu, wsd, wpg, wpp)


def _experts_kernel(be_ref, nblk_ref, xs_ref, wg_ref, wu_ref, wd_ref, y_ref, wgu_scr, wd_scr):
    b = pl.program_id(0)
    prev = be_ref[jnp.maximum(b - 1, 0)]

    @pl.when((b == 0) | (be_ref[b] != prev))
    def _():
        wgu_scr[:, :EFF] = wg_ref[0].astype(BF16)
        wgu_scr[:, EFF:] = wu_ref[0].astype(BF16)
        wd_scr[...] = wd_ref[0].astype(BF16)

    @pl.when(b < nblk_ref[0])
    def _():
        gu = _dot(xs_ref[...], wgu_scr[...])
        g, u = gu[:, :EFF], gu[:, EFF:]
        y_ref[...] = _dot((g * _sigmoid(g) * u).astype(BF16), wd_scr[...]).astype(BF16)


def _experts(tb, xs, wg, wu, wd):
    row_blk = lambda b, be, nb: (jnp.minimum(b, nb[0] - 1), 0)
    grid_spec = pltpu.PrefetchScalarGridSpec(
        num_scalar_prefetch=2,
        grid=(N_BLK,),
        in_specs=[
            pl.BlockSpec((RB, D), row_blk),
            pl.BlockSpec((1, D, EFF), lambda b, be, nb: (be[b], 0, 0)),
            pl.BlockSpec((1, D, EFF), lambda b, be, nb: (be[b], 0, 0)),
            pl.BlockSpec((1, EFF, D), lambda b, be, nb: (be[b], 0, 0)),
        ],
        out_specs=pl.BlockSpec((RB, D), row_blk),
        scratch_shapes=[pltpu.VMEM((D, 2 * EFF), BF16), pltpu.VMEM((EFF, D), BF16)],
    )
    return pl.pallas_call(
        _experts_kernel,
        grid_spec=grid_spec,
        out_shape=jax.ShapeDtypeStruct((R_TOT, D), BF16),
        compiler_params=pltpu.CompilerParams(
            dimension_semantics=("arbitrary",), vmem_limit_bytes=VMEM_LIMIT),
        name="experts",
    )(tb["block_e"], tb["nblk"], xs, wg, wu, wd)


def _combine_kernel(nch_ref, dst_ref, base_ref, yb_any, eidx_ref, gw_ref, rank_ref, loc_ref, g_ref, b_ref,
                    outp_ref, outs_ref, ybuf, sem):
    i = pl.program_id(0)
    nch = nch_ref[i]

    def chunk_copy(q):
        loc_row = pl.multiple_of(q * CH, CH)
        d = pl.multiple_of(dst_ref[i * N_CHUNK + q], CH)
        return pltpu.make_async_copy(yb_any.at[pl.ds(d, CH), :], ybuf.at[pl.ds(loc_row, CH), :], sem)

    def start(q, carry):
        chunk_copy(q).start()
        return carry

    lax.fori_loop(0, nch, start, 0)
    ldest = _local_dest(eidx_ref[0], rank_ref[0], loc_ref[0])
    gw = gw_ref[0]

    def wait(q, carry):
        chunk_copy(q).wait()
        return carry

    lax.fori_loop(0, nch, wait, 0)

    n_p = (nch * CH + PCH - 1) // PCH
    tail_rows = n_p * PCH - nch * CH

    def zero_tail(q, carry):
        r = pl.multiple_of(nch * CH + q * CH, CH)
        ybuf[pl.ds(r, CH), :] = jnp.zeros((CH, D), BF16)
        return carry

    lax.fori_loop(0, tail_rows // CH, zero_tail, 0)

    def perm_chunk(c, acc):
        r0 = pl.multiple_of(c * PCH, PCH)
        rid = lax.broadcasted_iota(I32, (PCH, TS), 0) + r0
        wmat = jnp.zeros((PCH, TS), F32)
        for k in range(TOP_K):
            wmat = jnp.where(rid == ldest[k], gw[k:k + 1, :], wmat)
        return acc + _dot_tn(wmat.astype(BF16), ybuf[pl.ds(r0, PCH), :])

    routed = lax.fori_loop(0, n_p, perm_chunk, jnp.zeros((TS, D), F32))
    res = _layer_norm(base_ref[...] + routed, g_ref[...], b_ref[...])

    @pl.when(i < N_TILES_P)
    def _():
        outp_ref[...] = res

    @pl.when(i >= N_TILES_P)
    def _():
        outs_ref[...] = res


def _combine(tb, base, yb, eidx, gw, rank, loc3, g, b):
    grid_spec = pltpu.PrefetchScalarGridSpec(
        num_scalar_prefetch=2,
        grid=(N_TILES,),
        in_specs=[
            pl.BlockSpec((TS, D), lambda i, *_: (i, 0)),
            pl.BlockSpec(memory_space=pl.ANY),
            pl.BlockSpec((1, 8, TS), lambda i, *_: (i, 0, 0)),
            pl.BlockSpec((1, 8, TS), lambda i, *_: (i, 0, 0)),
            pl.BlockSpec((1, 8, TS), lambda i, *_: (i, 0, 0)),
            pl.BlockSpec((1, N_EXP, 1), lambda i, *_: (i, 0, 0)),
            pl.BlockSpec(g.shape, lambda i, *_: (0, 0)),
            pl.BlockSpec(b.shape, lambda i, *_: (0, 0)),
        ],
        out_specs=[
            pl.BlockSpec((TS, D), lambda i, *_: (jnp.minimum(i, N_TILES_P - 1), 0)),
            pl.BlockSpec((TS, D), lambda i, *_: (jnp.maximum(i - N_TILES_P, 0), 0)),
        ],
        scratch_shapes=[pltpu.VMEM((R_LOC, D), BF16), pltpu.SemaphoreType.DMA(())],
    )
    return pl.pallas_call(
        _combine_kernel,
        grid_spec=grid_spec,
        out_shape=[jax.ShapeDtypeStruct((T_P, D), F32), jax.ShapeDtypeStruct((T_S, D), F32)],
        compiler_params=pltpu.CompilerParams(
            dimension_semantics=("arbitrary",), vmem_limit_bytes=VMEM_LIMIT),
        name="combine_ln2",
    )(tb["nch"], tb["dst"], base, yb, eidx, gw, rank, loc3, g, b)


def _q_perm():
    cols = []
    for p in range(8):
        cols += list(range(p * HD, (p + 1) * HD)) + list(range((8 + p) * HD, (9 + p) * HD))
    return np.asarray(cols, np.int32)


def kernel(x_prompt, x_sample, p_prompt, p_sample, state_win_k, state_win_v, state_conv, w_in, attn_sinks, conv_w, w_attn_out, w_conv_out, w_o, ln1_g, ln1_b, w_router, router_bias, w_e_gate, w_e_up, w_e_down, w_s_gate, w_s_up, w_s_down, w_ple_gate, w_ple_proj, ln2_g, ln2_b):
    perm = _q_perm()
    win = jnp.concatenate([w_in[0][:, perm], w_in[0][:, D:]], axis=1).astype(BF16)
    wa = w_attn_out[0][perm, :].astype(BF16)
    wc = w_conv_out[0].astype(BF16)
    wo = w_o[0].astype(BF16)
    sinks = attn_sinks[0].astype(F32)

    def sink_rows(rep):
        return jnp.stack([jnp.repeat(sinks[:8], rep), jnp.repeat(sinks[8:], rep)])[:, :, None]

    g1, b1 = ln1_g[0][None, :], ln1_b[0][None, :]
    g2, b2 = ln2_g[0][None, :], ln2_b[0][None, :]
    cw = conv_w[0]

    h_all, kp, vp, cp = _mixer_prompt(x_prompt, win, sink_rows(WINDOW), cw, wa, wc, wo, g1, b1)
    h_all, ks, vs, cs = _mixer_sample(
        h_all, x_sample,
        state_win_k[0].reshape(DEC_BATCH, WINDOW, KVW), state_win_v[0].reshape(DEC_BATCH, WINDOW, KVW),
        state_conv[0], win, sink_rows(DEC_SEQ), cw, wa, wc, wo, g1, b1)

    wr_t = w_router[0].T.astype(BF16)
    eidx, gw, rank, cnt = _router(h_all, wr_t, router_bias[0][:, None].astype(F32))
    tb = _dispatch_tables(cnt[:, 0, :].astype(I32))
    loc3 = tb["loc"][:, :, None].astype(F32)

    wsgu = jnp.concatenate([w_s_gate[0], w_s_up[0]], axis=1).astype(BF16)
    base, xs = _shared_dispatch(
        tb, h_all, p_prompt[0].reshape(T_P, PLE), p_sample[0].reshape(T_S, PLE), eidx, rank, loc3,
        wsgu, w_s_down[0].astype(BF16), w_ple_gate[0].astype(BF16), w_ple_proj[0].astype(BF16))
    yb = _experts(tb, xs, w_e_gate[0], w_e_up[0], w_e_down[0])
    yp, ys = _combine(tb, base, yb, eidx, gw, rank, loc3, g2, b2)

    return (yp.reshape(BATCH, SEQ, D), ys.reshape(DEC_BATCH, DEC_SEQ, D),
            kp.reshape(1, BATCH, WINDOW, N_KV, HD), vp.reshape(1, BATCH, WINDOW, N_KV, HD),
            cp[None],
            ks.reshape(1, DEC_BATCH, WINDOW, N_KV, HD), vs.reshape(1, DEC_BATCH, WINDOW, N_KV, HD),
            cs[None])
```
